```python
import math
import jax, jax.numpy as jnp
from jax import lax
import numpy as np

D_MODEL = 2048
BATCH = 4
SEQ = 2048
DEPTH = 4

CHUNK = 64
N_MIXERS = 2
EPS = 1e-6

GLA_HEADS = 4
GLA_KEY_WIDTH = D_MODEL // 2
GLA_VAL_WIDTH = D_MODEL
GLA_DK = GLA_KEY_WIDTH // GLA_HEADS
GLA_DV = GLA_VAL_WIDTH // GLA_HEADS
GLA_GATE_RANK = 16
GLA_GATE_TEMP = 16.0
GLA_IN_WIDTH = 2 * GLA_KEY_WIDTH + 2 * GLA_VAL_WIDTH + GLA_GATE_RANK

S5_WIDTH = D_MODEL // 2
S5_GROUP = 16
S5_GROUPS = S5_WIDTH // S5_GROUP
S5_STATE = 64
S5_DT_MIN = 1e-3
S5_DT_MAX = 1e-1
S5_EIG_CLIP = -1e-4

MLP_HIDDEN = 4 * D_MODEL

kernel_name = "hybrid_gla_s5_stream_block"


def _rmsnorm(x, g):
    xf = x.astype(jnp.float32)
    y = xf * lax.rsqrt(jnp.mean(xf * xf, axis=-1, keepdims=True) + EPS)
    return (y * g.astype(jnp.float32)).astype(x.dtype)


def _gla_mixer(h, w_in, w_gate_up, b_gate, o_norm, w_out):
    bsz, seq, _ = h.shape
    nc = seq // CHUNK
    proj = h @ w_in
    q, k, v, r, g_low = jnp.split(
        proj,
        [GLA_KEY_WIDTH, 2 * GLA_KEY_WIDTH, 2 * GLA_KEY_WIDTH + GLA_VAL_WIDTH,
         2 * GLA_KEY_WIDTH + 2 * GLA_VAL_WIDTH], axis=-1)
    log_a = jax.nn.log_sigmoid((g_low @ w_gate_up + b_gate).astype(jnp.float32)) / GLA_GATE_TEMP

    def to_chunks(t, dh):
        return t.reshape(bsz, nc, CHUNK, GLA_HEADS, dh).transpose(1, 0, 3, 2, 4).astype(jnp.float32)

    qc = to_chunks(q, GLA_DK) * (GLA_DK ** -0.5)
    kc = to_chunks(k, GLA_DK)
    vc = to_chunks(v, GLA_DV)
    ac = to_chunks(log_a, GLA_DK)
    cum = jnp.cumsum(ac, axis=3)
    total = cum[:, :, :, -1:, :]
    k_dec = kc * jnp.exp(total - cum)
    chunk_decay = jnp.exp(total[:, :, :, 0, :])

    def step(state, xs):
        q_c, k_c, v_c, d_c = xs
        state = d_c[..., None] * state + jnp.einsum('bhck,bhcv->bhkv', k_c, v_c)
        return state, jnp.einsum('bhck,bhkv->bhcv', q_c, state)

    s0 = jnp.zeros((bsz, GLA_HEADS, GLA_DK, GLA_DV), jnp.float32)
    _, o = lax.scan(step, s0, (qc, k_dec, vc, chunk_decay))
    o = o * lax.rsqrt(jnp.mean(o * o, axis=-1, keepdims=True) + EPS) * o_norm.astype(jnp.float32)
    o = o.transpose(1, 0, 3, 2, 4).reshape(bsz, seq, GLA_VAL_WIDTH).astype(h.dtype)
    return (o * jax.nn.silu(r)) @ w_out


def _s5_mixer(h, w_in, lam_re, lam_im, log_dt, b_re, b_im, c_re, c_im, d_skip, w_out):
    bsz, seq, _ = h.shape
    u = (h @ w_in).astype(jnp.float32)
    ug = u.reshape(bsz, seq, S5_GROUPS, S5_GROUP)
    lr = jnp.minimum(lam_re.astype(jnp.float32), S5_EIG_CLIP)
    li = lam_im.astype(jnp.float32)
    dt = jnp.exp(log_dt.astype(jnp.float32))[:, None]
    mag = jnp.exp(lr * dt)
    ang = li * dt
    ab_re = mag * jnp.cos(ang)
    ab_im = mag * jnp.sin(ang)
    den = lr * lr + li * li
    nr = ab_re - 1.0
    f_re = (nr * lr + ab_im * li) / den
    f_im = (ab_im * lr - nr * li) / den
    br = b_re.astype(jnp.float32)
    bi = b_im.astype(jnp.float32)
    bb_re = f_re[..., None] * br - f_im[..., None] * bi
    bb_im = f_re[..., None] * bi + f_im[..., None] * br
    bu_re = jnp.einsum('gnc,blgc->blgn', bb_re, ug)
    bu_im = jnp.einsum('gnc,blgc->blgn', bb_im, ug)
    a_re = jnp.broadcast_to(ab_re, bu_re.shape)
    a_im = jnp.broadcast_to(ab_im, bu_im.shape)

    def combine(e1, e2):
        a1r, a1i, b1r, b1i = e1
        a2r, a2i, b2r, b2i = e2
        return (a2r * a1r - a2i * a1i,
                a2r * a1i + a2i * a1r,
                a2r * b1r - a2i * b1i + b2r,
                a2r * b1i + a2i * b1r + b2i)

    _, _, x_re, x_im = lax.associative_scan(combine, (a_re, a_im, bu_re, bu_im), axis=1)
    y = (jnp.einsum('gcn,blgn->blgc', c_re.astype(jnp.float32), x_re)
         - jnp.einsum('gcn,blgn->blgc', c_im.astype(jnp.float32), x_im))
    y = y.reshape(bsz, seq, S5_WIDTH) + d_skip.astype(jnp.float32) * u
    y = jax.nn.gelu(y).astype(h.dtype)
    val, gate = jnp.split(y @ w_out, 2, axis=-1)
    return val * jax.nn.sigmoid(gate)


def _sq_relu_mlp(h, w_up, w_down):
    a = jax.nn.relu(h @ w_up)
    return (a * a) @ w_down


def setup_inputs(seed: int = 0) -> dict:
    key = jax.random.key(seed)
    ks = jax.random.split(key, 24)
    n_gla = len(range(0, DEPTH, N_MIXERS))
    n_s5 = len(range(1, DEPTH, N_MIXERS))
    res_scale = (2 * DEPTH) ** -0.5

    def nrm(k, shape, scale):
        return jax.random.normal(k, shape, jnp.float32) * scale

    def gain(k, shape):
        return 1.0 + 0.02 * jax.random.normal(k, shape, jnp.float32)

    lam_im0 = math.pi * jnp.arange(S5_STATE, dtype=jnp.float32)
    return {
        'x': jax.random.normal(ks[0], (BATCH, SEQ, D_MODEL), jnp.float32),
        'gla_norm': gain(ks[1], (n_gla, D_MODEL)),
        'gla_w_in': nrm(ks[2], (n_gla, D_MODEL, GLA_IN_WIDTH), D_MODEL ** -0.5),
        'gla_w_gate_up': nrm(ks[3], (n_gla, GLA_GATE_RANK, GLA_KEY_WIDTH), GLA_GATE_RANK ** -0.5),
        'gla_b_gate': 1.0 + 0.1 * jax.random.normal(ks[4], (n_gla, GLA_KEY_WIDTH), jnp.float32),
        'gla_o_norm': gain(ks[5], (n_gla, GLA_DV)),
        'gla_w_out': nrm(ks[6], (n_gla, GLA_VAL_WIDTH, D_MODEL), GLA_VAL_WIDTH ** -0.5 * res_scale),
        's5_norm': gain(ks[7], (n_s5, D_MODEL)),
        's5_w_in': nrm(ks[8], (n_s5, D_MODEL, S5_WIDTH), D_MODEL ** -0.5),
        's5_lam_re': -0.5 + 0.01 * jax.random.normal(ks[9], (n_s5, S5_GROUPS, S5_STATE), jnp.float32),
        's5_lam_im': lam_im0 + 0.01 * jax.random.normal(ks[10], (n_s5, S5_GROUPS, S5_STATE), jnp.float32),
        's5_log_dt': jax.random.uniform(ks[11], (n_s5, S5_GROUPS), jnp.float32,
                                        minval=math.log(S5_DT_MIN), maxval=math.log(S5_DT_MAX)),
        's5_b_re': nrm(ks[12], (n_s5, S5_GROUPS, S5_STATE, S5_GROUP), (2 * S5_GROUP) ** -0.5),
        's5_b_im': nrm(ks[13], (n_s5, S5_GROUPS, S5_STATE, S5_GROUP), (2 * S5_GROUP) ** -0.5),
        's5_c_re': nrm(ks[14], (n_s5, S5_GROUPS, S5_GROUP, S5_STATE), (2 * S5_STATE) ** -0.5),
        's5_c_im': nrm(ks[15], (n_s5, S5_GROUPS, S5_GROUP, S5_STATE), (2 * S5_STATE) ** -0.5),
        's5_d': jax.random.normal(ks[16], (n_s5, S5_WIDTH), jnp.float32),
        's5_w_out': nrm(ks[17], (n_s5, S5_WIDTH, 2 * D_MODEL), S5_WIDTH ** -0.5 * res_scale),
        'mlp_norm': gain(ks[18], (DEPTH, D_MODEL)),
        'mlp_w_up': nrm(ks[19], (DEPTH, D_MODEL, MLP_HIDDEN), D_MODEL ** -0.5),
        'mlp_w_down': nrm(ks[20], (DEPTH, MLP_HIDDEN, D_MODEL), MLP_HIDDEN ** -0.5 * res_scale),
        'final_norm': gain(ks[21], (D_MODEL,)),
    }


def reference(x, gla_norm, gla_w_in, gla_w_gate_up, gla_b_gate, gla_o_norm, gla_w_out,
              s5_norm, s5_w_in, s5_lam_re, s5_lam_im, s5_log_dt, s5_b_re, s5_b_im,
              s5_c_re, s5_c_im, s5_d, s5_w_out, mlp_norm, mlp_w_up, mlp_w_down, final_norm):
    h = x
    for i in range(DEPTH):
        j = i // N_MIXERS
        if i % N_MIXERS == 0:
            h = h + _gla_mixer(_rmsnorm(h, gla_norm[j]), gla_w_in[j], gla_w_gate_up[j],
                               gla_b_gate[j], gla_o_norm[j], gla_w_out[j])
        else:
            h = h + _s5_mixer(_rmsnorm(h, s5_norm[j]), s5_w_in[j], s5_lam_re[j], s5_lam_im[j],
                              s5_log_dt[j], s5_b_re[j], s5_b_im[j], s5_c_re[j], s5_c_im[j],
                              s5_d[j], s5_w_out[j])
        h = h + _sq_relu_mlp(_rmsnorm(h, mlp_norm[i]), mlp_w_up[i], mlp_w_down[i])
    return _rmsnorm(h, final_norm)
```

```python
import functools

import jax
import jax.numpy as jnp
import numpy as np
from jax import lax
from jax.experimental import pallas as pl
from jax.experimental.pallas import tpu as pltpu

F32 = jnp.float32
BF16 = jnp.bfloat16

EPS = 1e-6
GLA_CHUNK = 64
GLA_HEADS = 4
GLA_GATE_TEMP = 16.0
S5_GROUP = 16
S5_EIG_CLIP = -1e-4
N_MIXERS = 2

LANES = 128
S5_CS = 16
VMEM_LIMIT = 56 * 1024 * 1024


def _cparams(*sem):
    return pltpu.CompilerParams(dimension_semantics=sem, vmem_limit_bytes=VMEM_LIMIT)


def _rmsnorm(x, g):
    ms = jnp.mean(x * x, axis=-1, keepdims=True)
    return x * lax.rsqrt(ms + EPS) * g


def _dot(a, b):
    return jnp.dot(a, b, preferred_element_type=F32)


def _dot_nt(a, b):
    return lax.dot_general(a, b, (((1,), (1,)), ((), ())), preferred_element_type=F32)


def _dot_tn(a, b):
    return lax.dot_general(a, b, (((0,), (0,)), ((), ())), preferred_element_type=F32)


def _split_bf16(x):
    hi = x.astype(BF16)
    lo = (x - hi.astype(F32)).astype(BF16)
    return hi, lo


def _norm_matmul_kernel(x_ref, g_ref, w_ref, o_ref, hn_ref):
    @pl.when(pl.program_id(1) == 0)
    def _():
        hn_ref[...] = _rmsnorm(x_ref[...], g_ref[...]).astype(BF16)

    o_ref[...] = _dot(hn_ref[...], w_ref[...]).astype(o_ref.dtype)


def _norm_matmul2_kernel(x_ref, g_ref, w_ref, w2_ref, o_ref, o2_ref, hn_ref):
    @pl.when(pl.program_id(1) == 0)
    def _():
        hn = _rmsnorm(x_ref[...], g_ref[...]).astype(BF16)
        hn_ref[...] = hn
        o2_ref[...] = _dot(hn, w2_ref[...])

    o_ref[...] = _dot(hn_ref[...], w_ref[...]).astype(o_ref.dtype)


def _norm_matmul(x, g, w, w2=None, *, tm, tn):
    t, d = x.shape
    n = w.shape[1]
    grid = (t // tm, n // tn)
    x_spec = pl.BlockSpec((tm, d), lambda i, j: (i, 0))
    g_spec = pl.BlockSpec((1, d), lambda i, j: (0, 0))
    w_spec = pl.BlockSpec((d, tn), lambda i, j: (0, j))
    o_spec = pl.BlockSpec((tm, tn), lambda i, j: (i, j))
    scratch = [pltpu.VMEM((tm, d), BF16)]
    if w2 is None:
        return pl.pallas_call(
            _norm_matmul_kernel,
            grid=grid,
            in_specs=[x_spec, g_spec, w_spec],
            out_specs=o_spec,
            out_shape=jax.ShapeDtypeStruct((t, n), F32),
            scratch_shapes=scratch,
            compiler_params=_cparams("parallel", "arbitrary"),
            name="norm_matmul",
        )(x, g, w)
    n2 = w2.shape[1]
    return pl.pallas_call(
        _norm_matmul2_kernel,
        grid=grid,
        in_specs=[x_spec, g_spec, w_spec, pl.BlockSpec((d, n2), lambda i, j: (0, 0))],
        out_specs=[o_spec, pl.BlockSpec((tm, n2), lambda i, j: (i, 0))],
        out_shape=[jax.ShapeDtypeStruct((t, n), F32), jax.ShapeDtypeStruct((t, n2), F32)],
        scratch_shapes=scratch,
        compiler_params=_cparams("parallel", "arbitrary"),
        name="norm_matmul2",
    )(x, g, w, w2)


def _matmul_res_kernel(a_ref, w_ref, r_ref, o_ref):
    o_ref[...] = r_ref[...] + _dot(a_ref[...], w_ref[...])


def _matmul_res(a, w, res, *, tm, tn):
    t, k = a.shape
    n = w.shape[1]
    return pl.pallas_call(
        _matmul_res_kernel,
        grid=(t // tm, n // tn),
        in_specs=[
            pl.BlockSpec((tm, k), lambda i, j: (i, 0)),
            pl.BlockSpec((k, tn), lambda i, j: (0, j)),
            pl.BlockSpec((tm, tn), lambda i, j: (i, j)),
        ],
        out_specs=pl.BlockSpec((tm, tn), lambda i, j: (i, j)),
        out_shape=jax.ShapeDtypeStruct((t, n), F32),
        compiler_params=_cparams("parallel", "parallel"),
        name="matmul_res",
    )(a, w, res)


def _glu_res_kernel(a_ref, wv_ref, wg_ref, r_ref, o_ref):
    a = a_ref[...]
    val = _dot(a, wv_ref[...])
    gate = _dot(a, wg_ref[...])
    o_ref[...] = r_ref[...] + val * jax.nn.sigmoid(gate)


def _glu_res(a, w, res, *, tm, tn):
    t, k = a.shape
    n = w.shape[1] // 2
    nb = n // tn
    return pl.pallas_call(
        _glu_res_kernel,
        grid=(t // tm, nb),
        in_specs=[
            pl.BlockSpec((tm, k), lambda i, j: (i, 0)),
            pl.BlockSpec((k, tn), lambda i, j: (0, j)),
            pl.BlockSpec((k, tn), lambda i, j: (0, nb + j)),
            pl.BlockSpec((tm, tn), lambda i, j: (i, j)),
        ],
        out_specs=pl.BlockSpec((tm, tn), lambda i, j: (i, j)),
        out_shape=jax.ShapeDtypeStruct((t, n), F32),
        compiler_params=_cparams("parallel", "parallel"),
        name="glu_res",
    )(a, w, w, res)


def _mlp_kernel(x_ref, g_ref, wu_ref, wd_ref, fg_ref, o_ref, hn_ref, acc_ref, *, final_norm):
    j = pl.program_id(1)

    @pl.when(j == 0)
    def _():
        hn_ref[...] = _rmsnorm(x_ref[...], g_ref[...]).astype(BF16)
        acc_ref[...] = jnp.zeros_like(acc_ref)

    a = jnp.maximum(_dot(hn_ref[...], wu_ref[...]), 0.0)
    acc_ref[...] += _dot((a * a).astype(BF16), wd_ref[...])

    @pl.when(j == pl.num_programs(1) - 1)
    def _():
        y = x_ref[...] + acc_ref[...]
        if final_norm:
            y = _rmsnorm(y, fg_ref[...])
        o_ref[...] = y


def _mlp(x, g, wu, wd, fg, *, final_norm, tm, th):
    t, d = x.shape
    hid = wu.shape[1]
    return pl.pallas_call(
        functools.partial(_mlp_kernel, final_norm=final_norm),
        grid=(t // tm, hid // th),
        in_specs=[
            pl.BlockSpec((tm, d), lambda i, j: (i, 0)),
            pl.BlockSpec((1, d), lambda i, j: (0, 0)),
            pl.BlockSpec((d, th), lambda i, j: (0, j)),
            pl.BlockSpec((th, d), lambda i, j: (j, 0)),
            pl.BlockSpec((1, d), lambda i, j: (0, 0)),
        ],
        out_specs=pl.BlockSpec((tm, d), lambda i, j: (i, 0)),
        out_shape=jax.ShapeDtypeStruct((t, d), F32),
        scratch_shapes=[pltpu.VMEM((tm, d), BF16), pltpu.VMEM((tm, d), F32)],
        compiler_params=_cparams("parallel", "arbitrary"),
        name="mlp",
    )(x, g, wu, wd, fg)


def _gla_kernel(q_ref, k_ref, v_ref, r_ref, gl_ref, wg_ref, bg_ref, on_ref, o_ref, st_ref,
                *, n_chunks, q_scale):
    c = GLA_CHUNK
    st_ref[...] = jnp.zeros_like(st_ref)
    row = lax.broadcasted_iota(jnp.int32, (c, c), 0)
    col = lax.broadcasted_iota(jnp.int32, (c, c), 1)
    tri = jnp.where(col <= row, 1.0, 0.0).astype(BF16)
    wg = wg_ref[...]
    bg = bg_ref[...]
    on = on_ref[...]

    def body(ci, carry):
        rows = pl.ds(pl.multiple_of(ci * c, c), c)
        z = _dot(gl_ref[rows, :].astype(BF16), wg) + bg
        log_a = (jnp.minimum(z, 0.0) - jnp.log1p(jnp.exp(-jnp.abs(z)))) / GLA_GATE_TEMP
        la_hi, la_lo = _split_bf16(log_a)
        cum = _dot(tri, la_hi) + _dot(tri, la_lo)
        total = cum[c - 1:c, :]
        k_dec = (k_ref[rows, :] * jnp.exp(total - cum)).astype(BF16)
        decay = jnp.exp(total)
        st = decay * st_ref[...] + _dot_tn(v_ref[rows, :].astype(BF16), k_dec)
        st_ref[...] = st
        q = (q_ref[rows, :] * q_scale).astype(BF16)
        o = _dot_nt(q, st.astype(BF16))
        o = o * lax.rsqrt(jnp.mean(o * o, axis=-1, keepdims=True) + EPS) * on
        r = r_ref[rows, :]
        o_ref[rows, :] = (o * (r * jax.nn.sigmoid(r))).astype(o_ref.dtype)
        return carry

    lax.fori_loop(0, n_chunks, body, 0)


def _gla_chunks(p, glow, wg, bg, on, *, batch, seq, dk, dv):
    t = batch * seq
    kw = GLA_HEADS * dk
    vw = GLA_HEADS * dv
    assert p.shape == (t, 2 * kw + 2 * vw)
    assert kw % dk == 0 and (2 * kw) % dv == 0
    k_off = kw // dk
    v_off = 2 * kw // dv
    r_off = (2 * kw + vw) // dv
    gr = glow.shape[1]
    return pl.pallas_call(
        functools.partial(_gla_kernel, n_chunks=seq // GLA_CHUNK, q_scale=dk ** -0.5),
        grid=(batch, GLA_HEADS),
        in_specs=[
            pl.BlockSpec((seq, dk), lambda b, h: (b, h)),
            pl.BlockSpec((seq, dk), lambda b, h: (b, k_off + h)),
            pl.BlockSpec((seq, dv), lambda b, h: (b, v_off + h)),
            pl.BlockSpec((seq, dv), lambda b, h: (b, r_off + h)),
            pl.BlockSpec((seq, gr), lambda b, h: (b, 0)),
            pl.BlockSpec((gr, dk), lambda b, h: (0, h)),
            pl.BlockSpec((1, dk), lambda b, h: (0, h)),
            pl.BlockSpec((1, dv), lambda b, h: (0, 0)),
        ],
        out_specs=pl.BlockSpec((seq, dv), lambda b, h: (b, h)),
        out_shape=jax.ShapeDtypeStruct((t, vw), BF16),
        scratch_shapes=[pltpu.VMEM((dv, dk), F32)],
        compiler_params=_cparams("parallel", "parallel"),
        name="gla_chunks",
    )(p, p, p, p, glow, wg, bg, on)


def _s5_pow_table(cs, n_steps):
    ms = list(range(cs + 1))
    d = 1
    while d < n_steps:
        if cs * d not in ms:
            ms.append(cs * d)
        d *= 2
    pad = (-len(ms)) % 8
    return ms + [0] * pad


def _s5_kernel(*refs, cs, n_steps, pow_ms):
    u_refs = refs[:cs]
    (mt_ref, lr_ref, li_ref, ldt_ref, btr_ref, bti_ref, cr_ref, ci_ref, dsk_ref) = refs[cs:cs + 9]
    y_refs = refs[cs + 9:2 * cs + 9]
    w_ref, bc_ref, cct_ref, lhs_ref, q_ref = refs[2 * cs + 9:]
    ns = lr_ref.shape[-1]
    nc = LANES

    lr = jnp.minimum(lr_ref[...], S5_EIG_CLIP)
    li = li_ref[...]
    dt = jnp.exp(ldt_ref[...])
    base_r = lr * dt
    base_i = li * dt
    mt = mt_ref[...]
    mag = jnp.exp(mt * base_r)
    p_re = mag * jnp.cos(mt * base_i)
    p_im = mag * jnp.sin(mt * base_i)

    def apow(m):
        idx = pow_ms.index(m)
        return p_re[idx:idx + 1, :], p_im[idx:idx + 1, :]

    a_re, a_im = apow(1)
    den = lr * lr + li * li
    nr = a_re - 1.0
    f_re = (nr * lr + a_im * li) / den
    f_im = (a_im * lr - nr * li) / den
    btr = btr_ref[...]
    bti = bti_ref[...]
    bd_re = f_re * btr - f_im * bti
    bd_im = f_re * bti + f_im * btr
    c_re = cr_ref[...]
    c_im = ci_ref[...]

    for tau in range(cs):
        pr, pi = apow(tau)
        q_re = bd_re * pr - bd_im * pi
        q_im = bd_re * pi + bd_im * pr
        q_ref[0, tau * nc:(tau + 1) * nc, :] = q_re
        q_ref[1, tau * nc:(tau + 1) * nc, :] = q_im
        s = cs - 1 - tau
        bc_ref[s * nc:(s + 1) * nc, 0:ns] = q_re.astype(BF16)
        bc_ref[s * nc:(s + 1) * nc, ns:2 * ns] = q_im.astype(BF16)

    qr_hi, qr_lo = _split_bf16(q_ref[0])
    qi_hi, qi_lo = _split_bf16(q_ref[1])
    cr_hi, cr_lo = _split_bf16(c_re)
    ci_hi, ci_lo = _split_bf16(c_im)
    kall = (_dot_nt(qr_hi, cr_hi) + _dot_nt(qr_hi, cr_lo) + _dot_nt(qr_lo, cr_hi)
            - _dot_nt(qi_hi, ci_hi) - _dot_nt(qi_hi, ci_lo) - _dot_nt(qi_lo, ci_hi))
    kall = kall.astype(BF16)

    zero_blk = jnp.zeros((nc, nc), BF16)
    for s in range(cs):
        for t in range(cs):
            blk = kall[(t - s) * nc:(t - s + 1) * nc, :] if t >= s else zero_blk
            w_ref[s * nc:(s + 1) * nc, t * nc:(t + 1) * nc] = blk

    for t in range(cs):
        pr, pi = apow(t + 1)
        cct_ref[t * nc:(t + 1) * nc, 0:ns] = (c_re * pr - c_im * pi).astype(BF16)
        cct_ref[t * nc:(t + 1) * nc, ns:2 * ns] = (-(c_re * pi + c_im * pr)).astype(BF16)

    for s in range(cs):
        lhs_ref[:, s * nc:(s + 1) * nc] = u_refs[s][...].astype(BF16)
    lhs = lhs_ref[...]

    inj = _dot(lhs, bc_ref[...])
    i_re = inj[:, 0:ns]
    i_im = inj[:, ns:2 * ns]
    n_rows = inj.shape[0]
    step = lax.broadcasted_iota(jnp.int32, (n_rows, ns), 0) % n_steps

    def shifted(x, d):
        return jnp.where(step >= d, pltpu.roll(x, d, 0), 0.0)

    d = 1
    while d < n_steps:
        pr, pi = apow(cs * d)
        s_re = shifted(i_re, d)
        s_im = shifted(i_im, d)
        i_re, i_im = (i_re + (pr * s_re - pi * s_im), i_im + (pr * s_im + pi * s_re))
        d *= 2
    x_re = shifted(i_re, 1)
    x_im = shifted(i_im, 1)
    x = jnp.concatenate([x_re, x_im], axis=1).astype(BF16)

    y = _dot(lhs, w_ref[...]) + _dot_nt(x, cct_ref[...])
    dsk = dsk_ref[...]
    for t in range(cs):
        yt = y[:, t * nc:(t + 1) * nc] + dsk * u_refs[t][...]
        y_refs[t][...] = jax.nn.gelu(yt).astype(BF16)


def _block_diag_groups(x, gpb):
    g, r, n = x.shape
    nb = g // gpb
    x4 = x.reshape(nb, gpb, r, 1, n)
    eye = jnp.eye(gpb, dtype=bool).reshape(1, gpb, 1, gpb, 1)
    return jnp.where(eye, x4, 0.0).reshape(nb, gpb * r, gpb * n)


def _s5_core(u, lam_re, lam_im, log_dt, b_re, b_im, c_re, c_im, d_skip, *, batch, seq):
    t, width = u.shape
    groups, n_state = lam_re.shape
    cs = S5_CS
    gpb = LANES // S5_GROUP
    nblk = width // LANES
    ns = gpb * n_state
    n_rows = t // cs
    n_steps = seq // cs
    pow_ms = _s5_pow_table(cs, n_steps)

    u2 = u.reshape(n_rows, cs * width)
    mtab = jnp.asarray(np.repeat(np.asarray(pow_ms, np.float32)[:, None], ns, axis=1))
    lr = lam_re.reshape(nblk, 1, ns)
    li = lam_im.reshape(nblk, 1, ns)
    ldt = jnp.repeat(log_dt, n_state).reshape(nblk, 1, ns)
    btr = _block_diag_groups(jnp.swapaxes(b_re, 1, 2), gpb)
    bti = _block_diag_groups(jnp.swapaxes(b_im, 1, 2), gpb)
    cr = _block_diag_groups(c_re, gpb)
    ci = _block_diag_groups(c_im, gpb)
    dsk = d_skip.reshape(nblk, 1, LANES)

    def u_spec(s):
        return pl.BlockSpec((n_rows, LANES), lambda j: (0, s * nblk + j))

    row_spec = pl.BlockSpec((None, 1, ns), lambda j: (j, 0, 0))
    mat_spec = pl.BlockSpec((None, LANES, ns), lambda j: (j, 0, 0))
    ys = pl.pallas_call(
        functools.partial(_s5_kernel, cs=cs, n_steps=n_steps, pow_ms=pow_ms),
        grid=(nblk,),
        in_specs=[u_spec(s) for s in range(cs)] + [
            pl.BlockSpec((len(pow_ms), ns), lambda j: (0, 0)),
            row_spec, row_spec, row_spec,
            mat_spec, mat_spec, mat_spec, mat_spec,
            pl.BlockSpec((None, 1, LANES), lambda j: (j, 0, 0)),
        ],
        out_specs=[pl.BlockSpec((n_rows, LANES), lambda j: (0, j)) for _ in range(cs)],
        out_shape=[jax.ShapeDtypeStruct((n_rows, width), BF16) for _ in range(cs)],
        scratch_shapes=[
            pltpu.VMEM((cs * LANES, cs * LANES), BF16),
            pltpu.VMEM((cs * LANES, 2 * ns), BF16),
            pltpu.VMEM((cs * LANES, 2 * ns), BF16),
            pltpu.VMEM((n_rows, cs * LANES), BF16),
            pltpu.VMEM((2, cs * LANES, ns), F32),
        ],
        compiler_params=_cparams("parallel"),
        name="s5_core",
    )(*([u2] * cs), mtab, lr, li, ldt, btr, bti, cr, ci, dsk)
    return jnp.stack(ys, axis=1).reshape(t, width)


def kernel(x, gla_norm, gla_w_in, gla_w_gate_up, gla_b_gate, gla_o_norm, gla_w_out, s5_norm, s5_w_in, s5_lam_re, s5_lam_im, s5_log_dt, s5_b_re, s5_b_im, s5_c_re, s5_c_im, s5_d, s5_w_out, mlp_norm, mlp_w_up, mlp_w_down, final_norm):
    batch, seq, d = x.shape
    depth = mlp_norm.shape[0]
    t = batch * seq
    gate_rank, kw = gla_w_gate_up.shape[1:]
    dv = gla_o_norm.shape[1]
    dk = kw // GLA_HEADS
    main_w = gla_w_in.shape[2] - gate_rank

    h = x.reshape(t, d)
    fg = final_norm.reshape(1, d)
    for i in range(depth):
        j = i // N_MIXERS
        if i % N_MIXERS == 0:
            w_in = gla_w_in[j]
            w_main = w_in[:, :main_w].astype(BF16)
            w_glow = jnp.pad(w_in[:, main_w:], ((0, 0), (0, LANES - gate_rank))).astype(BF16)
            w_gate = jnp.pad(gla_w_gate_up[j], ((0, LANES - gate_rank), (0, 0))).astype(BF16)
            p, glow = _norm_matmul(h, gla_norm[j].reshape(1, d), w_main, w_glow, tm=512, tn=1024)
            og = _gla_chunks(p, glow, w_gate, gla_b_gate[j].reshape(1, kw),
                             gla_o_norm[j].reshape(1, dv), batch=batch, seq=seq, dk=dk, dv=dv)
            h = _matmul_res(og, gla_w_out[j].astype(BF16), h, tm=1024, tn=1024)
        else:
            u = _norm_matmul(h, s5_norm[j].reshape(1, d), s5_w_in[j].astype(BF16), tm=1024, tn=1024)
            y = _s5_core(u, s5_lam_re[j], s5_lam_im[j], s5_log_dt[j], s5_b_re[j], s5_b_im[j],
                         s5_c_re[j], s5_c_im[j], s5_d[j], batch=batch, seq=seq)
            h = _glu_res(y, s5_w_out[j].astype(BF16), h, tm=1024, tn=1024)
        h = _mlp(h, mlp_norm[i].reshape(1, d), mlp_w_up[i].astype(BF16), mlp_w_down[i].astype(BF16),
                 fg, final_norm=(i == depth - 1), tm=512, th=1024)
    return h.reshape(batch, seq, d)
```

```python
import functools

import jax
import jax.numpy as jnp
import numpy as np
from jax import lax
from jax.experimental import pallas as pl
from jax.experimental.pallas import tpu as pltpu

F32 = jnp.float32
BF16 = jnp.bfloat16

EPS = 1e-6
GLA_CHUNK = 64
GLA_HEADS = 4
GLA_GATE_TEMP = 16.0
S5_GROUP = 16
S5_EIG_CLIP = -1e-4
N_MIXERS = 2

LANES = 128
MXU_DIM = 256
S5_CS = 16
VMEM_LIMIT = 56 * 1024 * 1024


def _cparams(*sem):
    return pltpu.CompilerParams(dimension_semantics=sem, vmem_limit_bytes=VMEM_LIMIT)


def _rmsnorm(x, g):
    ms = jnp.mean(x * x, axis=-1, keepdims=True)
    return x * lax.rsqrt(ms + EPS) * g


def _dot(a, b):
    return jnp.dot(a, b, preferred_element_type=F32)


def _dot_nt(a, b):
    return lax.dot_general(a, b, (((1,), (1,)), ((), ())), preferred_element_type=F32)


def _dot_tn(a, b):
    return lax.dot_general(a, b, (((0,), (0,)), ((), ())), preferred_element_type=F32)


def _split_bf16(x):
    hi = x.astype(BF16)
    lo = (x - hi.astype(F32)).astype(BF16)
    return hi, lo


def _norm_matmul_kernel(x_ref, g_ref, w_ref, o_ref, hn_ref):
    @pl.when(pl.program_id(1) == 0)
    def _():
        hn_ref[...] = _rmsnorm(x_ref[...], g_ref[...]).astype(BF16)

    o_ref[...] = _dot(hn_ref[...], w_ref[...].astype(BF16))


def _norm_matmul(x, g, w, layer, *, tm, tn):
    t, d = x.shape
    n = w.shape[2]
    return pl.pallas_call(
        _norm_matmul_kernel,
        grid=(t // tm, n // tn),
        in_specs=[
            pl.BlockSpec((tm, d), lambda i, j: (i, 0)),
            pl.BlockSpec((1, d), lambda i, j: (0, 0)),
            pl.BlockSpec((None, d, tn), lambda i, j: (layer, 0, j)),
        ],
        out_specs=pl.BlockSpec((tm, tn), lambda i, j: (i, j)),
        out_shape=jax.ShapeDtypeStruct((t, n), F32),
        scratch_shapes=[pltpu.VMEM((tm, d), BF16)],
        compiler_params=_cparams("parallel", "arbitrary"),
        name="norm_matmul",
    )(x, g, w)


def _matmul_res_kernel(a_ref, w_ref, r_ref, o_ref):
    o_ref[...] = r_ref[...] + _dot(a_ref[...], w_ref[...].astype(BF16))


def _matmul_res(a, w, layer, res, *, tm, tn):
    t, k = a.shape
    n = w.shape[2]
    return pl.pallas_call(
        _matmul_res_kernel,
        grid=(n // tn, t // tm),
        in_specs=[
            pl.BlockSpec((tm, k), lambda j, i: (i, 0)),
            pl.BlockSpec((None, k, tn), lambda j, i: (layer, 0, j)),
            pl.BlockSpec((tm, tn), lambda j, i: (i, j)),
        ],
        out_specs=pl.BlockSpec((tm, tn), lambda j, i: (i, j)),
        out_shape=jax.ShapeDtypeStruct((t, n), F32),
        compiler_params=_cparams("parallel", "parallel"),
        name="matmul_res",
    )(a, w, res)


def _glu_res_kernel(a_ref, wv_ref, wg_ref, r_ref, o_ref):
    a = a_ref[...]
    val = _dot(a, wv_ref[...].astype(BF16))
    gate = _dot(a, wg_ref[...].astype(BF16))
    o_ref[...] = r_ref[...] + val * jax.nn.sigmoid(gate)


def _glu_res(a, w, layer, res, *, tn):
    cs, n_rows, k = a.shape
    n = w.shape[2] // 2
    nb = n // tn
    assert res.shape == (n_rows, cs * n)
    return pl.pallas_call(
        _glu_res_kernel,
        grid=(nb, cs),
        in_specs=[
            pl.BlockSpec((None, n_rows, k), lambda j, i: (i, 0, 0)),
            pl.BlockSpec((None, k, tn), lambda j, i: (layer, 0, j)),
            pl.BlockSpec((None, k, tn), lambda j, i: (layer, 0, nb + j)),
            pl.BlockSpec((n_rows, tn), lambda j, i: (0, i * nb + j)),
        ],
        out_specs=pl.BlockSpec((n_rows, tn), lambda j, i: (0, i * nb + j)),
        out_shape=jax.ShapeDtypeStruct(res.shape, F32),
        compiler_params=_cparams("parallel", "parallel"),
        name="glu_res",
    )(a, w, w, res)


def _mlp_kernel(x_ref, g_ref, wu_ref, wd_ref, fg_ref, o_ref, hn_ref, *, final_norm):
    j = pl.program_id(1)

    @pl.when(j == 0)
    def _():
        hn_ref[...] = _rmsnorm(x_ref[...], g_ref[...]).astype(BF16)
        o_ref[...] = jnp.zeros_like(o_ref)

    a = jnp.maximum(_dot(hn_ref[...], wu_ref[...].astype(BF16)), 0.0)
    o_ref[...] += _dot((a * a).astype(BF16), wd_ref[...].astype(BF16))

    @pl.when(j == pl.num_programs(1) - 1)
    def _():
        y = x_ref[...] + o_ref[...]
        if final_norm:
            y = _rmsnorm(y, fg_ref[...])
        o_ref[...] = y


def _mlp(x, g, wu, wd, layer, fg, *, final_norm, tm, th):
    t, d = x.shape
    hid = wu.shape[2]
    return pl.pallas_call(
        functools.partial(_mlp_kernel, final_norm=final_norm),
        grid=(t // tm, hid // th),
        in_specs=[
            pl.BlockSpec((tm, d), lambda i, j: (i, 0), pipeline_mode=pl.Buffered(1)),
            pl.BlockSpec((1, d), lambda i, j: (0, 0)),
            pl.BlockSpec((None, d, th), lambda i, j: (layer, 0, j)),
            pl.BlockSpec((None, th, d), lambda i, j: (layer, j, 0)),
            pl.BlockSpec((1, d), lambda i, j: (0, 0)),
        ],
        out_specs=pl.BlockSpec((tm, d), lambda i, j: (i, 0)),
        out_shape=jax.ShapeDtypeStruct((t, d), F32),
        scratch_shapes=[pltpu.VMEM((tm, d), BF16)],
        compiler_params=_cparams("parallel", "arbitrary"),
        name="mlp",
    )(x, g, wu, wd, fg)


def _gla_kernel(x_ref, g_ref, wq_ref, wk_ref, wv_ref, wr_ref, wgl_ref, wg_ref, bg_ref, on_ref,
                o_ref, st_ref, *, q_scale):
    c = GLA_CHUNK
    nb, rows, d = x_ref.shape

    @pl.when(pl.program_id(1) == 0)
    def _():
        st_ref[...] = jnp.zeros_like(st_ref)

    hn = _rmsnorm(x_ref[...].reshape(nb * rows, d), g_ref[...]).astype(BF16)
    g_low = _dot(hn, wgl_ref[...].astype(BF16)).astype(BF16)
    k = _dot(hn, wk_ref[...].astype(BF16))
    v = _dot(hn, wv_ref[...].astype(BF16)).astype(BF16)
    q = (_dot(hn, wq_ref[...].astype(BF16)) * q_scale).astype(BF16)
    r = _dot(hn, wr_ref[...].astype(BF16))

    z = _dot(g_low, wg_ref[...].astype(BF16)) + bg_ref[...]
    log_a = (jnp.minimum(z, 0.0) - jnp.log1p(jnp.exp(-jnp.abs(z)))) / GLA_GATE_TEMP

    blk = MXU_DIM
    ri = lax.broadcasted_iota(jnp.int32, (blk, blk), 0)
    ci = lax.broadcasted_iota(jnp.int32, (blk, blk), 1)
    sfx_mat = jnp.where((ri // c == ci // c) & (ci > ri), 1.0, 0.0).astype(BF16)
    la_hi, la_lo = _split_bf16(log_a)
    sfx = jnp.concatenate(
        [_dot(sfx_mat, la_hi[i:i + blk]) + _dot(sfx_mat, la_lo[i:i + blk])
         for i in range(0, nb * rows, blk)], axis=0)
    k_dec = (k * jnp.exp(sfx)).astype(BF16)
    decay = jnp.exp(sfx + log_a)

    st = [st_ref[b] for b in range(nb)]
    outs = [[] for _ in range(nb)]
    for i in range(0, rows, c):
        for b in range(nb):
            lo = b * rows + i
            st[b] = decay[lo:lo + 1] * st[b] + _dot_tn(v[lo:lo + c], k_dec[lo:lo + c])
            outs[b].append(_dot_nt(q[lo:lo + c], st[b].astype(BF16)))
    for b in range(nb):
        st_ref[b] = st[b]
    o = jnp.concatenate([blk_o for per_b in outs for blk_o in per_b], axis=0)
    o = o * lax.rsqrt(jnp.mean(o * o, axis=-1, keepdims=True) + EPS) * on_ref[...]
    o = (o * (r * jax.nn.sigmoid(r))).astype(o_ref.dtype)
    o_ref[...] = o.reshape(nb, rows, o.shape[-1])


def _gla_mix(x, g, w_in, w_glow, w_gate, b_gate, o_norm, layer, *, dk, dv, rows):
    batch, seq, d = x.shape
    kw = GLA_HEADS * dk
    vw = GLA_HEADS * dv
    assert kw % dk == 0 and (2 * kw) % dv == 0 and seq % rows == 0 and rows % GLA_CHUNK == 0
    assert (batch * rows) % MXU_DIM == 0
    k_off = kw // dk
    v_off = 2 * kw // dv
    r_off = (2 * kw + vw) // dv
    gr = w_glow.shape[1]
    return pl.pallas_call(
        functools.partial(_gla_kernel, q_scale=dk ** -0.5),
        grid=(GLA_HEADS, seq // rows),
        in_specs=[
            pl.BlockSpec((batch, rows, d), lambda h, s: (0, s, 0)),
            pl.BlockSpec((1, d), lambda h, s: (0, 0)),
            pl.BlockSpec((None, d, dk), lambda h, s: (layer, 0, h)),
            pl.BlockSpec((None, d, dk), lambda h, s: (layer, 0, k_off + h)),
            pl.BlockSpec((None, d, dv), lambda h, s: (layer, 0, v_off + h)),
            pl.BlockSpec((None, d, dv), lambda h, s: (layer, 0, r_off + h)),
            pl.BlockSpec((d, gr), lambda h, s: (0, 0)),
            pl.BlockSpec((gr, dk), lambda h, s: (0, h)),
            pl.BlockSpec((1, dk), lambda h, s: (0, h)),
            pl.BlockSpec((1, dv), lambda h, s: (0, 0)),
        ],
        out_specs=pl.BlockSpec((batch, rows, dv), lambda h, s: (0, s, h)),
        out_shape=jax.ShapeDtypeStruct((batch, seq, vw), BF16),
        scratch_shapes=[pltpu.VMEM((batch, dv, dk), F32)],
        compiler_params=_cparams("parallel", "arbitrary"),
        name="gla_mix",
    )(x, g, w_in, w_in, w_in, w_in, w_glow, w_gate, b_gate, o_norm)


def _s5_pow_table(cs, n_steps):
    ms = list(range(cs + 1))
    d = 1
    while d < n_steps:
        if cs * d not in ms:
            ms.append(cs * d)
        d *= 2
    pad = (-len(ms)) % 8
    return ms + [0] * pad


def _s5_kernel(*refs, cs, n_steps, pow_ms):
    u_refs = refs[:cs]
    (mt_ref, lr_ref, li_ref, ldt_ref, btr_ref, bti_ref, cr_ref, ci_ref, dsk_ref, y_ref,
     w_ref, bc_ref, cct_ref, lhs_ref, q_ref) = refs[cs:]
    ns = lr_ref.shape[-1]
    nc = LANES

    lr = jnp.minimum(lr_ref[...], S5_EIG_CLIP)
    li = li_ref[...]
    dt = jnp.exp(ldt_ref[...])
    base_r = lr * dt
    base_i = li * dt
    mt = mt_ref[...]
    mag = jnp.exp(mt * base_r)
    p_re = mag * jnp.cos(mt * base_i)
    p_im = mag * jnp.sin(mt * base_i)

    def apow(m):
        idx = pow_ms.index(m)
        return p_re[idx:idx + 1, :], p_im[idx:idx + 1, :]

    a_re, a_im = apow(1)
    den = lr * lr + li * li
    nr = a_re - 1.0
    f_re = (nr * lr + a_im * li) / den
    f_im = (a_im * lr - nr * li) / den
    btr = btr_ref[...]
    bti = bti_ref[...]
    bd_re = f_re * btr - f_im * bti
    bd_im = f_re * bti + f_im * btr
    c_re = cr_ref[...]
    c_im = ci_ref[...]

    for tau in range(cs):
        pr, pi = apow(tau)
        q_re = bd_re * pr - bd_im * pi
        q_im = bd_re * pi + bd_im * pr
        q_ref[0, tau * nc:(tau + 1) * nc, :] = q_re
        q_ref[1, tau * nc:(tau + 1) * nc, :] = q_im
        s = cs - 1 - tau
        bc_ref[s * nc:(s + 1) * nc, 0:ns] = q_re.astype(BF16)
        bc_ref[s * nc:(s + 1) * nc, ns:2 * ns] = q_im.astype(BF16)

    qr_hi, qr_lo = _split_bf16(q_ref[0])
    qi_hi, qi_lo = _split_bf16(q_ref[1])
    cr_hi, cr_lo = _split_bf16(c_re)
    ci_hi, ci_lo = _split_bf16(c_im)
    kall = (_dot_nt(qr_hi, cr_hi) + _dot_nt(qr_hi, cr_lo) + _dot_nt(qr_lo, cr_hi)
            - _dot_nt(qi_hi, ci_hi) - _dot_nt(qi_hi, ci_lo) - _dot_nt(qi_lo, ci_hi))
    kall = kall.astype(BF16)

    zero_blk = jnp.zeros((nc, nc), BF16)
    for s in range(cs):
        for t in range(cs):
            blk = kall[(t - s) * nc:(t - s + 1) * nc, :] if t >= s else zero_blk
            w_ref[s * nc:(s + 1) * nc, t * nc:(t + 1) * nc] = blk

    for t in range(cs):
        pr, pi = apow(t + 1)
        cct_ref[t * nc:(t + 1) * nc, 0:ns] = (c_re * pr - c_im * pi).astype(BF16)
        cct_ref[t * nc:(t + 1) * nc, ns:2 * ns] = (-(c_re * pi + c_im * pr)).astype(BF16)

    for s in range(cs):
        lhs_ref[:, s * nc:(s + 1) * nc] = u_refs[s][...].astype(BF16)
    lhs = lhs_ref[...]

    inj = _dot(lhs, bc_ref[...])
    i_re = inj[:, 0:ns]
    i_im = inj[:, ns:2 * ns]
    n_rows = inj.shape[0]
    step = lax.broadcasted_iota(jnp.int32, (n_rows, ns), 0) % n_steps

    def shifted(x, d):
        return jnp.where(step >= d, pltpu.roll(x, d, 0), 0.0)

    d = 1
    while d < n_steps:
        pr, pi = apow(cs * d)
        s_re = shifted(i_re, d)
        s_im = shifted(i_im, d)
        i_re, i_im = (i_re + (pr * s_re - pi * s_im), i_im + (pr * s_im + pi * s_re))
        d *= 2
    x_re = shifted(i_re, 1)
    x_im = shifted(i_im, 1)
    x = jnp.concatenate([x_re, x_im], axis=1).astype(BF16)

    y = _dot(lhs, w_ref[...]) + _dot_nt(x, cct_ref[...])
    dsk = dsk_ref[...]
    for t in range(cs):
        yt = y[:, t * nc:(t + 1) * nc] + dsk * u_refs[t][...]
        y_ref[t] = jax.nn.gelu(yt).astype(BF16)


def _block_diag_groups(x, gpb):
    g, r, n = x.shape
    nb = g // gpb
    x4 = x.reshape(nb, gpb, r, 1, n)
    eye = jnp.eye(gpb, dtype=bool).reshape(1, gpb, 1, gpb, 1)
    return jnp.where(eye, x4, 0.0).reshape(nb, gpb * r, gpb * n)


def _s5_core(u2, lam_re, lam_im, log_dt, b_re, b_im, c_re, c_im, d_skip, *, seq):
    cs = S5_CS
    n_rows = u2.shape[0]
    width = u2.shape[1] // cs
    n_state = lam_re.shape[1]
    gpb = LANES // S5_GROUP
    nblk = width // LANES
    ns = gpb * n_state
    n_steps = seq // cs
    pow_ms = _s5_pow_table(cs, n_steps)

    mtab = jnp.asarray(np.repeat(np.asarray(pow_ms, np.float32)[:, None], ns, axis=1))
    lr = lam_re.reshape(nblk, 1, ns)
    li = lam_im.reshape(nblk, 1, ns)
    ldt = jnp.repeat(log_dt, n_state).reshape(nblk, 1, ns)
    btr = _block_diag_groups(jnp.swapaxes(b_re, 1, 2), gpb)
    bti = _block_diag_groups(jnp.swapaxes(b_im, 1, 2), gpb)
    cr = _block_diag_groups(c_re, gpb)
    ci = _block_diag_groups(c_im, gpb)
    dsk = d_skip.reshape(nblk, 1, LANES)

    def u_spec(s):
        return pl.BlockSpec((n_rows, LANES), lambda j: (0, s * nblk + j))

    row_spec = pl.BlockSpec((None, 1, ns), lambda j: (j, 0, 0))
    mat_spec = pl.BlockSpec((None, LANES, ns), lambda j: (j, 0, 0))
    return pl.pallas_call(
        functools.partial(_s5_kernel, cs=cs, n_steps=n_steps, pow_ms=pow_ms),
        grid=(nblk,),
        in_specs=[u_spec(s) for s in range(cs)] + [
            pl.BlockSpec((len(pow_ms), ns), lambda j: (0, 0)),
            row_spec, row_spec, row_spec,
            mat_spec, mat_spec, mat_spec, mat_spec,
            pl.BlockSpec((None, 1, LANES), lambda j: (j, 0, 0)),
        ],
        out_specs=pl.BlockSpec((cs, n_rows, LANES), lambda j: (0, 0, j)),
        out_shape=jax.ShapeDtypeStruct((cs, n_rows, width), BF16),
        scratch_shapes=[
            pltpu.VMEM((cs * LANES, cs * LANES), BF16),
            pltpu.VMEM((cs * LANES, 2 * ns), BF16),
            pltpu.VMEM((cs * LANES, 2 * ns), BF16),
            pltpu.VMEM((n_rows, cs * LANES), BF16),
            pltpu.VMEM((2, cs * LANES, ns), F32),
        ],
        compiler_params=_cparams("parallel"),
        name="s5_core",
    )(*([u2] * cs), mtab, lr, li, ldt, btr, bti, cr, ci, dsk)


def kernel(x, gla_norm, gla_w_in, gla_w_gate_up, gla_b_gate, gla_o_norm, gla_w_out, s5_norm, s5_w_in, s5_lam_re, s5_lam_im, s5_log_dt, s5_b_re, s5_b_im, s5_c_re, s5_c_im, s5_d, s5_w_out, mlp_norm, mlp_w_up, mlp_w_down, final_norm):
    batch, seq, d = x.shape
    depth = mlp_norm.shape[0]
    t = batch * seq
    gate_rank, kw = gla_w_gate_up.shape[1:]
    dv = gla_o_norm.shape[1]
    dk = kw // GLA_HEADS
    main_w = gla_w_in.shape[2] - gate_rank
    s5_w = s5_w_in.shape[2]

    h = x.reshape(t, d)
    fg = final_norm.reshape(1, d)
    for i in range(depth):
        j = i // N_MIXERS
        if i % N_MIXERS == 0:
            w_glow = jnp.pad(gla_w_in[j, :, main_w:], ((0, 0), (0, LANES - gate_rank)))
            w_gate = jnp.pad(gla_w_gate_up[j], ((0, LANES - gate_rank), (0, 0)))
            og = _gla_mix(h.reshape(batch, seq, d), gla_norm[j].reshape(1, d), gla_w_in, w_glow, w_gate,
                          gla_b_gate[j].reshape(1, kw), gla_o_norm[j].reshape(1, dv), j,
                          dk=dk, dv=dv, rows=128)
            h = _matmul_res(og.reshape(t, GLA_HEADS * dv), gla_w_out, j, h, tm=1024, tn=1024)
        else:
            u = _norm_matmul(h, s5_norm[j].reshape(1, d), s5_w_in, j, tm=1024, tn=1024)
            y = _s5_core(u.reshape(t // S5_CS, S5_CS * s5_w), s5_lam_re[j], s5_lam_im[j], s5_log_dt[j],
                         s5_b_re[j], s5_b_im[j], s5_c_re[j], s5_c_im[j], s5_d[j], seq=seq)
            h = _glu_res(y, s5_w_out, j, h.reshape(t // S5_CS, S5_CS * d), tn=1024).reshape(t, d)
        h = _mlp(h, mlp_norm[i].reshape(1, d), mlp_w_up, mlp_w_down, i, fg,
                 final_norm=(i == depth - 1), tm=1024, th=512)
    return h.reshape(batch, seq, d)
```

```python
import functools

import jax
import jax.numpy as jnp
import numpy as np
from jax import lax
from jax.experimental import pallas as pl
from jax.experimental.pallas import tpu as pltpu

F32 = jnp.float32
BF16 = jnp.bfloat16

EPS = 1e-6
GLA_CHUNK = 64
GLA_HEADS = 4
GLA_GATE_TEMP = 16.0
S5_GROUP = 16
S5_EIG_CLIP = -1e-4
N_MIXERS = 2

LANES = 128
MXU_DIM = 256
S5_CS = 16
VMEM_LIMIT = 56 * 1024 * 1024


def _cparams(*sem):
    return pltpu.CompilerParams(dimension_semantics=sem, vmem_limit_bytes=VMEM_LIMIT)


def _rmsnorm(x, g):
    ms = jnp.mean(x * x, axis=-1, keepdims=True)
    return x * lax.rsqrt(ms + EPS) * g


def _dot(a, b):
    return jnp.dot(a, b, preferred_element_type=F32)


def _dot_nt(a, b):
    return lax.dot_general(a, b, (((1,), (1,)), ((), ())), preferred_element_type=F32)


def _dot_tn(a, b):
    return lax.dot_general(a, b, (((0,), (0,)), ((), ())), preferred_element_type=F32)


def _split_bf16(x):
    hi = x.astype(BF16)
    lo = (x - hi.astype(F32)).astype(BF16)
    return hi, lo


def _norm_matmul_cm_kernel(x_ref, g_ref, w_ref, o_ref, y_ref):
    tk, cs, d = x_ref.shape
    hn = _rmsnorm(x_ref[...], g_ref[...]).astype(BF16).reshape(tk * cs, d)
    y = _dot(hn, w_ref[...].astype(BF16))
    y_ref[...] = y.reshape(tk, cs, y.shape[-1])
    for i in range(cs):
        o_ref[i] = y_ref[:, i, :]


def _norm_matmul_cm(x3, g, w, layer, *, tk):
    n_rows, cs, d = x3.shape
    n = w.shape[2]
    return pl.pallas_call(
        _norm_matmul_cm_kernel,
        grid=(n_rows // tk,),
        in_specs=[
            pl.BlockSpec((tk, cs, d), lambda i: (i, 0, 0)),
            pl.BlockSpec((1, d), lambda i: (0, 0)),
            pl.BlockSpec((None, d, n), lambda i: (layer, 0, 0)),
        ],
        out_specs=pl.BlockSpec((cs, tk, n), lambda i: (0, i, 0)),
        out_shape=jax.ShapeDtypeStruct((cs, n_rows, n), F32),
        scratch_shapes=[pltpu.VMEM((tk, cs, n), F32)],
        compiler_params=_cparams("parallel"),
        name="norm_matmul_cm",
    )(x3, g, w)


def _matmul_res_kernel(a_ref, w_ref, r_ref, o_ref):
    o_ref[...] = r_ref[...] + _dot(a_ref[...], w_ref[...].astype(BF16))


def _matmul_res(a, w, layer, res, *, tm, tn):
    t, k = a.shape
    n = w.shape[2]
    return pl.pallas_call(
        _matmul_res_kernel,
        grid=(n // tn, t // tm),
        in_specs=[
            pl.BlockSpec((tm, k), lambda j, i: (i, 0)),
            pl.BlockSpec((None, k, tn), lambda j, i: (layer, 0, j)),
            pl.BlockSpec((tm, tn), lambda j, i: (i, j)),
        ],
        out_specs=pl.BlockSpec((tm, tn), lambda j, i: (i, j)),
        out_shape=jax.ShapeDtypeStruct((t, n), F32),
        compiler_params=_cparams("parallel", "parallel"),
        name="matmul_res",
    )(a, w, res)


def _glu_res_kernel(a_ref, wv_ref, wg_ref, r_ref, o_ref):
    cs, tk, k = a_ref.shape
    a = a_ref[...].reshape(cs * tk, k)
    val = _dot(a, wv_ref[...].astype(BF16))
    gate = _dot(a, wg_ref[...].astype(BF16))
    res = val * jax.nn.sigmoid(gate)
    for i in range(cs):
        o_ref[:, i, :] = r_ref[:, i, :] + res[i * tk:(i + 1) * tk]


def _glu_res(a, w, layer, res3, *, tk, tn):
    cs, n_rows, k = a.shape
    n = w.shape[2] // 2
    nb = n // tn
    assert res3.shape == (n_rows, cs, n)
    return pl.pallas_call(
        _glu_res_kernel,
        grid=(nb, n_rows // tk),
        in_specs=[
            pl.BlockSpec((cs, tk, k), lambda j, i: (0, i, 0)),
            pl.BlockSpec((None, k, tn), lambda j, i: (layer, 0, j)),
            pl.BlockSpec((None, k, tn), lambda j, i: (layer, 0, nb + j)),
            pl.BlockSpec((tk, cs, tn), lambda j, i: (i, 0, j)),
        ],
        out_specs=pl.BlockSpec((tk, cs, tn), lambda j, i: (i, 0, j)),
        out_shape=jax.ShapeDtypeStruct(res3.shape, F32),
        compiler_params=_cparams("parallel", "parallel"),
        name="glu_res",
    )(a, w, w, res3)


def _mlp_kernel(x_ref, g_ref, wu_ref, wd_ref, fg_ref, o_ref, hn_ref, *, final_norm):
    j = pl.program_id(1)

    @pl.when(j == 0)
    def _():
        hn_ref[...] = _rmsnorm(x_ref[...], g_ref[...]).astype(BF16)
        o_ref[...] = jnp.zeros_like(o_ref)

    a = jnp.maximum(_dot(hn_ref[...], wu_ref[...].astype(BF16)), 0.0)
    o_ref[...] += _dot((a * a).astype(BF16), wd_ref[...].astype(BF16))

    @pl.when(j == pl.num_programs(1) - 1)
    def _():
        y = x_ref[...] + o_ref[...]
        if final_norm:
            y = _rmsnorm(y, fg_ref[...])
        o_ref[...] = y


def _mlp(x, g, wu, wd, layer, fg, *, final_norm, tm, th):
    t, d = x.shape
    hid = wu.shape[2]
    return pl.pallas_call(
        functools.partial(_mlp_kernel, final_norm=final_norm),
        grid=(t // tm, hid // th),
        in_specs=[
            pl.BlockSpec((tm, d), lambda i, j: (i, 0), pipeline_mode=pl.Buffered(1)),
            pl.BlockSpec((1, d), lambda i, j: (0, 0)),
            pl.BlockSpec((None, d, th), lambda i, j: (layer, 0, j)),
            pl.BlockSpec((None, th, d), lambda i, j: (layer, j, 0)),
            pl.BlockSpec((1, d), lambda i, j: (0, 0)),
        ],
        out_specs=pl.BlockSpec((tm, d), lambda i, j: (i, 0)),
        out_shape=jax.ShapeDtypeStruct((t, d), F32),
        scratch_shapes=[pltpu.VMEM((tm, d), BF16)],
        compiler_params=_cparams("parallel", "arbitrary"),
        name="mlp",
    )(x, g, wu, wd, fg)


def _gla_kernel(x_ref, g_ref, wq_ref, wk_ref, wv_ref, wr_ref, wgl_ref, wg_ref, bg_ref, on_ref,
                o_ref, st_ref, wq_s, wk_s, wv_s, wr_s, wgl_s, *, q_scale):
    c = GLA_CHUNK
    nb, rows, d = x_ref.shape
    assert rows == 2 * c and MXU_DIM % rows == 0

    @pl.when(pl.program_id(1) == 0)
    def _():
        st_ref[...] = jnp.zeros_like(st_ref)
        for src, dst in ((wq_ref, wq_s), (wk_ref, wk_s), (wv_ref, wv_s), (wr_ref, wr_s),
                         (wgl_ref, wgl_s)):
            dst[...] = src[...].astype(BF16)

    hn = _rmsnorm(x_ref[...].reshape(nb * rows, d), g_ref[...]).astype(BF16)
    g_low = _dot_nt(hn, wgl_s[...]).astype(BF16)
    k = _dot_nt(hn, wk_s[...])
    z = _dot(g_low, wg_ref[...].astype(BF16)) + bg_ref[...]
    log_a = (jnp.minimum(z, 0.0) - jnp.log1p(jnp.exp(-jnp.abs(z)))) / GLA_GATE_TEMP

    blk = MXU_DIM
    ri = lax.broadcasted_iota(jnp.int32, (blk, blk), 0)
    ci = lax.broadcasted_iota(jnp.int32, (blk, blk), 1)
    same_tile = ri // rows == ci // rows
    la_hi, la_lo = _split_bf16(log_a)

    def masked_sum(mask):
        m = jnp.where(mask, 1.0, 0.0).astype(BF16)
        return jnp.concatenate([_dot(m, la_hi[i:i + blk]) + _dot(m, la_lo[i:i + blk])
                                for i in range(0, nb * rows, blk)], axis=0)

    sfx_chunk = masked_sum((ri // c == ci // c) & (ci > ri))
    sfx_tile = masked_sum(same_tile & (ci > ri))
    pfx = masked_sum(same_tile & (ci // c <= ri // c))
    kd = (k * jnp.exp(sfx_chunk)).astype(BF16)
    kt = (k * jnp.exp(sfx_tile)).astype(BF16)
    v = _dot_nt(hn, wv_s[...]).astype(BF16)
    q_f = _dot_nt(hn, wq_s[...]) * q_scale
    q = q_f.astype(BF16)
    qt = (q_f * jnp.exp(pfx)).astype(BF16)
    r = _dot_nt(hn, wr_s[...])

    row_c = lax.broadcasted_iota(jnp.int32, (rows, rows), 0) // c
    col_c = lax.broadcasted_iota(jnp.int32, (rows, rows), 1) // c
    outs = []
    for b in range(nb):
        sl = slice(b * rows, (b + 1) * rows)
        s_in = st_ref[b]
        a = jnp.where(row_c == col_c, _dot_nt(q[sl], kd[sl]),
                      jnp.where(row_c > col_c, _dot_nt(q[sl], kt[sl]), 0.0))
        outs.append(_dot_nt(qt[sl], s_in.astype(BF16)) + _dot(a.astype(BF16), v[sl]))
        dec = jnp.exp(pfx[(b + 1) * rows - 1:(b + 1) * rows])
        st_ref[b] = dec * s_in + _dot_tn(v[sl], kt[sl])
    o = jnp.concatenate(outs, axis=0)
    o = o * lax.rsqrt(jnp.mean(o * o, axis=-1, keepdims=True) + EPS) * on_ref[...]
    o = (o * (r * jax.nn.sigmoid(r))).astype(o_ref.dtype)
    o_ref[...] = o.reshape(nb, rows, o.shape[-1])


def _gla_mix(x, g, w_in_t, w_glow_t, w_gate, b_gate, o_norm, layer, *, dk, dv, rows):
    batch, seq, d = x.shape
    kw = GLA_HEADS * dk
    vw = GLA_HEADS * dv
    assert kw % dk == 0 and (2 * kw) % dv == 0 and seq % rows == 0 and rows % GLA_CHUNK == 0
    assert (batch * rows) % MXU_DIM == 0
    k_off = kw // dk
    v_off = 2 * kw // dv
    r_off = (2 * kw + vw) // dv
    gr = w_glow_t.shape[0]
    return pl.pallas_call(
        functools.partial(_gla_kernel, q_scale=dk ** -0.5),
        grid=(GLA_HEADS, seq // rows),
        in_specs=[
            pl.BlockSpec((batch, rows, d), lambda h, s: (0, s, 0)),
            pl.BlockSpec((1, d), lambda h, s: (0, 0)),
            pl.BlockSpec((None, dk, d), lambda h, s: (layer, h, 0)),
            pl.BlockSpec((None, dk, d), lambda h, s: (layer, k_off + h, 0)),
            pl.BlockSpec((None, dv, d), lambda h, s: (layer, v_off + h, 0)),
            pl.BlockSpec((None, dv, d), lambda h, s: (layer, r_off + h, 0)),
            pl.BlockSpec((gr, d), lambda h, s: (0, 0)),
            pl.BlockSpec((gr, dk), lambda h, s: (0, h)),
            pl.BlockSpec((1, dk), lambda h, s: (0, h)),
            pl.BlockSpec((1, dv), lambda h, s: (0, 0)),
        ],
        out_specs=pl.BlockSpec((batch, rows, dv), lambda h, s: (0, s, h)),
        out_shape=jax.ShapeDtypeStruct((batch, seq, vw), BF16),
        scratch_shapes=[
            pltpu.VMEM((batch, dv, dk), F32),
            pltpu.VMEM((dk, d), BF16),
            pltpu.VMEM((dk, d), BF16),
            pltpu.VMEM((dv, d), BF16),
            pltpu.VMEM((dv, d), BF16),
            pltpu.VMEM((gr, d), BF16),
        ],
        compiler_params=_cparams("parallel", "arbitrary"),
        name="gla_mix",
    )(x, g, w_in_t, w_in_t, w_in_t, w_in_t, w_glow_t, w_gate, b_gate, o_norm)


def _s5_pow_table(cs, n_steps):
    ms = list(range(cs + 1))
    d = 1
    while d < n_steps:
        if cs * d not in ms:
            ms.append(cs * d)
        d *= 2
    pad = (-len(ms)) % 8
    return ms + [0] * pad


def _s5_kernel(u_ref, mt_ref, lr_ref, li_ref, ldt_ref, btr_ref, bti_ref, cr_ref, ci_ref, dsk_ref,
               y_ref, w_ref, bc_ref, cct_ref, lhs_ref, q_ref, *, n_steps, pow_ms):
    cs = u_ref.shape[0]
    ns = lr_ref.shape[-1]
    nc = LANES

    lr = jnp.minimum(lr_ref[...], S5_EIG_CLIP)
    li = li_ref[...]
    dt = jnp.exp(ldt_ref[...])
    base_r = lr * dt
    base_i = li * dt
    mt = mt_ref[...]
    mag = jnp.exp(mt * base_r)
    p_re = mag * jnp.cos(mt * base_i)
    p_im = mag * jnp.sin(mt * base_i)

    def apow(m):
        idx = pow_ms.index(m)
        return p_re[idx:idx + 1, :], p_im[idx:idx + 1, :]

    a_re, a_im = apow(1)
    den = lr * lr + li * li
    nr = a_re - 1.0
    f_re = (nr * lr + a_im * li) / den
    f_im = (a_im * lr - nr * li) / den
    btr = btr_ref[...]
    bti = bti_ref[...]
    bd_re = f_re * btr - f_im * bti
    bd_im = f_re * bti + f_im * btr
    c_re = cr_ref[...]
    c_im = ci_ref[...]

    for tau in range(cs):
        pr, pi = apow(tau)
        q_re = bd_re * pr - bd_im * pi
        q_im = bd_re * pi + bd_im * pr
        q_ref[0, tau * nc:(tau + 1) * nc, :] = q_re
        q_ref[1, tau * nc:(tau + 1) * nc, :] = q_im
        s = cs - 1 - tau
        bc_ref[s * nc:(s + 1) * nc, 0:ns] = q_re.astype(BF16)
        bc_ref[s * nc:(s + 1) * nc, ns:2 * ns] = q_im.astype(BF16)

    qr_hi, qr_lo = _split_bf16(q_ref[0])
    qi_hi, qi_lo = _split_bf16(q_ref[1])
    cr_hi, cr_lo = _split_bf16(c_re)
    ci_hi, ci_lo = _split_bf16(c_im)
    kall = (_dot_nt(qr_hi, cr_hi) + _dot_nt(qr_hi, cr_lo) + _dot_nt(qr_lo, cr_hi)
            - _dot_nt(qi_hi, ci_hi) - _dot_nt(qi_hi, ci_lo) - _dot_nt(qi_lo, ci_hi))
    kall = kall.astype(BF16)

    zero_blk = jnp.zeros((nc, nc), BF16)
    for s in range(cs):
        for t in range(cs):
            blk = kall[(t - s) * nc:(t - s + 1) * nc, :] if t >= s else zero_blk
            w_ref[s * nc:(s + 1) * nc, t * nc:(t + 1) * nc] = blk

    for t in range(cs):
        pr, pi = apow(t + 1)
        cct_ref[t * nc:(t + 1) * nc, 0:ns] = (c_re * pr - c_im * pi).astype(BF16)
        cct_ref[t * nc:(t + 1) * nc, ns:2 * ns] = (-(c_re * pi + c_im * pr)).astype(BF16)

    for s in range(cs):
        lhs_ref[:, s * nc:(s + 1) * nc] = u_ref[s].astype(BF16)
    lhs = lhs_ref[...]

    inj = _dot(lhs, bc_ref[...])
    i_re = inj[:, 0:ns]
    i_im = inj[:, ns:2 * ns]
    n_rows = inj.shape[0]
    step = lax.broadcasted_iota(jnp.int32, (n_rows, ns), 0) % n_steps

    def shifted(x, d):
        return jnp.where(step >= d, pltpu.roll(x, d, 0), 0.0)

    d = 1
    while d < n_steps:
        pr, pi = apow(cs * d)
        s_re = shifted(i_re, d)
        s_im = shifted(i_im, d)
        i_re, i_im = (i_re + (pr * s_re - pi * s_im), i_im + (pr * s_im + pi * s_re))
        d *= 2
    x_re = shifted(i_re, 1)
    x_im = shifted(i_im, 1)
    x = jnp.concatenate([x_re, x_im], axis=1).astype(BF16)

    y = _dot(lhs, w_ref[...]) + _dot_nt(x, cct_ref[...])
    dsk = dsk_ref[...]
    for t in range(cs):
        yt = y[:, t * nc:(t + 1) * nc] + dsk * u_ref[t]
        y_ref[t] = jax.nn.gelu(yt).astype(BF16)


def _block_diag_groups(x, gpb):
    g, r, n = x.shape
    nb = g // gpb
    x4 = x.reshape(nb, gpb, r, 1, n)
    eye = jnp.eye(gpb, dtype=bool).reshape(1, gpb, 1, gpb, 1)
    return jnp.where(eye, x4, 0.0).reshape(nb, gpb * r, gpb * n)


def _s5_core(u, lam_re, lam_im, log_dt, b_re, b_im, c_re, c_im, d_skip, *, seq):
    cs, n_rows, width = u.shape
    n_state = lam_re.shape[1]
    gpb = LANES // S5_GROUP
    nblk = width // LANES
    ns = gpb * n_state
    n_steps = seq // cs
    pow_ms = _s5_pow_table(cs, n_steps)

    mtab = jnp.asarray(np.repeat(np.asarray(pow_ms, np.float32)[:, None], ns, axis=1))
    lr = lam_re.reshape(nblk, 1, ns)
    li = lam_im.reshape(nblk, 1, ns)
    ldt = jnp.repeat(log_dt, n_state).reshape(nblk, 1, ns)
    btr = _block_diag_groups(jnp.swapaxes(b_re, 1, 2), gpb)
    bti = _block_diag_groups(jnp.swapaxes(b_im, 1, 2), gpb)
    cr = _block_diag_groups(c_re, gpb)
    ci = _block_diag_groups(c_im, gpb)
    dsk = d_skip.reshape(nblk, 1, LANES)

    row_spec = pl.BlockSpec((None, 1, ns), lambda j: (j, 0, 0))
    mat_spec = pl.BlockSpec((None, LANES, ns), lambda j: (j, 0, 0))
    return pl.pallas_call(
        functools.partial(_s5_kernel, n_steps=n_steps, pow_ms=pow_ms),
        grid=(nblk,),
        in_specs=[
            pl.BlockSpec((cs, n_rows, LANES), lambda j: (0, 0, j)),
            pl.BlockSpec((len(pow_ms), ns), lambda j: (0, 0)),
            row_spec, row_spec, row_spec,
            mat_spec, mat_spec, mat_spec, mat_spec,
            pl.BlockSpec((None, 1, LANES), lambda j: (j, 0, 0)),
        ],
        out_specs=pl.BlockSpec((cs, n_rows, LANES), lambda j: (0, 0, j)),
        out_shape=jax.ShapeDtypeStruct((cs, n_rows, width), BF16),
        scratch_shapes=[
            pltpu.VMEM((cs * LANES, cs * LANES), BF16),
            pltpu.VMEM((cs * LANES, 2 * ns), BF16),
            pltpu.VMEM((cs * LANES, 2 * ns), BF16),
            pltpu.VMEM((n_rows, cs * LANES), BF16),
            pltpu.VMEM((2, cs * LANES, ns), F32),
        ],
        compiler_params=_cparams("parallel"),
        name="s5_core",
    )(u, mtab, lr, li, ldt, btr, bti, cr, ci, dsk)


def kernel(x, gla_norm, gla_w_in, gla_w_gate_up, gla_b_gate, gla_o_norm, gla_w_out, s5_norm, s5_w_in, s5_lam_re, s5_lam_im, s5_log_dt, s5_b_re, s5_b_im, s5_c_re, s5_c_im, s5_d, s5_w_out, mlp_norm, mlp_w_up, mlp_w_down, final_norm):
    batch, seq, d = x.shape
    depth = mlp_norm.shape[0]
    t = batch * seq
    cs = S5_CS
    gate_rank, kw = gla_w_gate_up.shape[1:]
    dv = gla_o_norm.shape[1]
    dk = kw // GLA_HEADS
    main_w = gla_w_in.shape[2] - gate_rank

    gla_w_in_t = jnp.swapaxes(gla_w_in, 1, 2)
    h = x.reshape(t, d)
    fg = final_norm.reshape(1, d)
    for i in range(depth):
        j = i // N_MIXERS
        if i % N_MIXERS == 0:
            w_glow_t = jnp.pad(gla_w_in_t[j, main_w:, :], ((0, LANES - gate_rank), (0, 0)))
            w_gate = jnp.pad(gla_w_gate_up[j], ((0, LANES - gate_rank), (0, 0)))
            og = _gla_mix(h.reshape(batch, seq, d), gla_norm[j].reshape(1, d), gla_w_in_t, w_glow_t, w_gate,
                          gla_b_gate[j].reshape(1, kw), gla_o_norm[j].reshape(1, dv), j,
                          dk=dk, dv=dv, rows=128)
            h = _matmul_res(og.reshape(t, GLA_HEADS * dv), gla_w_out, j, h, tm=1024, tn=1024)
        else:
            h3 = h.reshape(t // cs, cs, d)
            u = _norm_matmul_cm(h3, s5_norm[j].reshape(1, d), s5_w_in, j, tk=64)
            y = _s5_core(u, s5_lam_re[j], s5_lam_im[j], s5_log_dt[j],
                         s5_b_re[j], s5_b_im[j], s5_c_re[j], s5_c_im[j], s5_d[j], seq=seq)
            h = _glu_res(y, s5_w_out, j, h3, tk=64, tn=1024).reshape(t, d)
        h = _mlp(h, mlp_norm[i].reshape(1, d), mlp_w_up, mlp_w_down, i, fg,
                 final_norm=(i == depth - 1), tm=1024, th=512)
    return h.reshape(batch, seq, d)
```

```python
import functools

import jax
import jax.numpy as jnp
import numpy as np
from jax import lax
from jax.experimental import pallas as pl
from jax.experimental.pallas import tpu as pltpu

F32 = jnp.float32
BF16 = jnp.bfloat16

EPS = 1e-6
GLA_CHUNK = 64
GLA_HEADS = 4
GLA_GATE_TEMP = 16.0
S5_GROUP = 16
S5_EIG_CLIP = -1e-4
N_MIXERS = 2

LANES = 128
MXU_DIM = 256
S5_CS = 8
VMEM_LIMIT = 61 * 1024 * 1024


def _cparams(*sem):
    return pltpu.CompilerParams(dimension_semantics=sem, vmem_limit_bytes=VMEM_LIMIT)


def _rmsnorm(x, g):
    ms = jnp.mean(x * x, axis=-1, keepdims=True)
    return x * lax.rsqrt(ms + EPS) * g


def _dot(a, b):
    return jnp.dot(a, b, preferred_element_type=F32)


def _dot_nt(a, b):
    return lax.dot_general(a, b, (((1,), (1,)), ((), ())), preferred_element_type=F32)


def _dot_tn(a, b):
    return lax.dot_general(a, b, (((0,), (0,)), ((), ())), preferred_element_type=F32)


def _split_bf16(x):
    hi = x.astype(BF16)
    lo = (x - hi.astype(F32)).astype(BF16)
    return hi, lo


def _norm_matmul_cm_kernel(x_ref, g_ref, w_ref, o_ref, y_ref):
    tk, cs, d = x_ref.shape
    hn = _rmsnorm(x_ref[...], g_ref[...]).astype(BF16).reshape(tk * cs, d)
    y = _dot(hn, w_ref[...].astype(BF16))
    y_ref[...] = y.reshape(tk, cs, y.shape[-1])
    for i in range(cs):
        o_ref[i] = y_ref[:, i, :]


def _norm_matmul_cm(x3, g, w, layer, *, tk):
    n_rows, cs, d = x3.shape
    n = w.shape[2]
    return pl.pallas_call(
        _norm_matmul_cm_kernel,
        grid=(n_rows // tk,),
        in_specs=[
            pl.BlockSpec((tk, cs, d), lambda i: (i, 0, 0)),
            pl.BlockSpec((1, d), lambda i: (0, 0)),
            pl.BlockSpec((None, d, n), lambda i: (layer, 0, 0)),
        ],
        out_specs=pl.BlockSpec((cs, tk, n), lambda i: (0, i, 0)),
        out_shape=jax.ShapeDtypeStruct((cs, n_rows, n), F32),
        scratch_shapes=[pltpu.VMEM((tk, cs, n), F32)],
        compiler_params=_cparams("parallel"),
        name="norm_matmul_cm",
    )(x3, g, w)


def _matmul_res_kernel(a_ref, w_ref, r_ref, o_ref):
    o_ref[...] = r_ref[...] + _dot(a_ref[...], w_ref[...].astype(BF16))


def _matmul_res(a, w, layer, res, *, tm, tn):
    t, k = a.shape
    n = w.shape[2]
    return pl.pallas_call(
        _matmul_res_kernel,
        grid=(n // tn, t // tm),
        in_specs=[
            pl.BlockSpec((tm, k), lambda j, i: (i, 0)),
            pl.BlockSpec((None, k, tn), lambda j, i: (layer, 0, j)),
            pl.BlockSpec((tm, tn), lambda j, i: (i, j)),
        ],
        out_specs=pl.BlockSpec((tm, tn), lambda j, i: (i, j)),
        out_shape=jax.ShapeDtypeStruct((t, n), F32),
        compiler_params=_cparams("parallel", "parallel"),
        name="matmul_res",
    )(a, w, res)


def _glu_res_kernel(a_ref, wv_ref, wg_ref, r_ref, o_ref):
    cs, tk, k = a_ref.shape
    a = a_ref[...].reshape(cs * tk, k)
    val = _dot(a, wv_ref[...].astype(BF16))
    gate = _dot(a, wg_ref[...].astype(BF16))
    res = val * jax.nn.sigmoid(gate)
    for i in range(cs):
        o_ref[:, i, :] = r_ref[:, i, :] + res[i * tk:(i + 1) * tk]


def _glu_res(a, w, layer, res3, *, tk, tn):
    cs, n_rows, k = a.shape
    n = w.shape[2] // 2
    nb = n // tn
    assert res3.shape == (n_rows, cs, n)
    return pl.pallas_call(
        _glu_res_kernel,
        grid=(nb, n_rows // tk),
        in_specs=[
            pl.BlockSpec((cs, tk, k), lambda j, i: (0, i, 0)),
            pl.BlockSpec((None, k, tn), lambda j, i: (layer, 0, j)),
            pl.BlockSpec((None, k, tn), lambda j, i: (layer, 0, nb + j)),
            pl.BlockSpec((tk, cs, tn), lambda j, i: (i, 0, j)),
        ],
        out_specs=pl.BlockSpec((tk, cs, tn), lambda j, i: (i, 0, j)),
        out_shape=jax.ShapeDtypeStruct(res3.shape, F32),
        compiler_params=_cparams("parallel", "parallel"),
        name="glu_res",
    )(a, w, w, res3)


def _mlp_kernel(x_ref, g_ref, wu_ref, wd_ref, fg_ref, o_ref, hn_ref, *, final_norm):
    j = pl.program_id(1)

    def hidden_tile(hn):
        a = jnp.maximum(_dot(hn, wu_ref[...].astype(BF16)), 0.0)
        return _dot((a * a).astype(BF16), wd_ref[...].astype(BF16))

    @pl.when(j == 0)
    def _():
        hn = _rmsnorm(x_ref[...], g_ref[...]).astype(BF16)
        hn_ref[...] = hn
        o_ref[...] = hidden_tile(hn)

    @pl.when(j > 0)
    def _():
        o_ref[...] += hidden_tile(hn_ref[...])

    @pl.when(j == pl.num_programs(1) - 1)
    def _():
        y = x_ref[...] + o_ref[...]
        if final_norm:
            y = _rmsnorm(y, fg_ref[...])
        o_ref[...] = y


def _mlp(x, g, wu, wd, layer, fg, *, final_norm, tm, th):
    t, d = x.shape
    hid = wu.shape[2]
    return pl.pallas_call(
        functools.partial(_mlp_kernel, final_norm=final_norm),
        grid=(t // tm, hid // th),
        in_specs=[
            pl.BlockSpec((tm, d), lambda i, j: (i, 0)),
            pl.BlockSpec((1, d), lambda i, j: (0, 0)),
            pl.BlockSpec((None, d, th), lambda i, j: (layer, 0, j)),
            pl.BlockSpec((None, th, d), lambda i, j: (layer, j, 0)),
            pl.BlockSpec((1, d), lambda i, j: (0, 0)),
        ],
        out_specs=pl.BlockSpec((tm, d), lambda i, j: (i, 0)),
        out_shape=jax.ShapeDtypeStruct((t, d), F32),
        scratch_shapes=[pltpu.VMEM((tm, d), BF16)],
        compiler_params=_cparams("parallel", "arbitrary"),
        name="mlp",
    )(x, g, wu, wd, fg)


def _gla_kernel(x_ref, g_ref, wq_ref, wk_ref, wv_ref, wr_ref, wgl_ref, wg_ref, bg_ref, on_ref,
                o_ref, st_ref, wq_s, wk_s, wv_s, wr_s, wgl_s, *, q_scale):
    c = GLA_CHUNK
    nb, rows, d = x_ref.shape
    assert rows == 2 * c and MXU_DIM % rows == 0

    @pl.when(pl.program_id(1) == 0)
    def _():
        st_ref[...] = jnp.zeros_like(st_ref)
        for src, dst in ((wq_ref, wq_s), (wk_ref, wk_s), (wv_ref, wv_s), (wr_ref, wr_s),
                         (wgl_ref, wgl_s)):
            dst[...] = src[...].astype(BF16)

    hn = _rmsnorm(x_ref[...].reshape(nb * rows, d), g_ref[...]).astype(BF16)
    g_low = _dot_nt(hn, wgl_s[...]).astype(BF16)
    k = _dot_nt(hn, wk_s[...])
    z = _dot(g_low, wg_ref[...].astype(BF16)) + bg_ref[...]
    log_a = (jnp.minimum(z, 0.0) - jnp.log1p(jnp.exp(-jnp.abs(z)))) / GLA_GATE_TEMP

    blk = MXU_DIM
    ri = lax.broadcasted_iota(jnp.int32, (blk, blk), 0)
    ci = lax.broadcasted_iota(jnp.int32, (blk, blk), 1)
    mask = jnp.where((ri // rows == ci // rows) & (ci > ri), 1.0, 0.0).astype(BF16)
    la_hi, la_lo = _split_bf16(log_a)
    sfx_tile = jnp.concatenate([_dot(mask, la_hi[i:i + blk]) + _dot(mask, la_lo[i:i + blk])
                                for i in range(0, nb * rows, blk)], axis=0)
    t1_rows, t0_rows, tt_rows = [], [], []
    for b in range(nb):
        lo = b * rows
        t1 = sfx_tile[lo + c - 1:lo + c]
        tt = sfx_tile[lo:lo + 1] + log_a[lo:lo + 1]
        t1_rows.append(jnp.broadcast_to(t1, (rows, t1.shape[1])))
        t0_rows.append(jnp.broadcast_to(tt - t1, (rows, t1.shape[1])))
        tt_rows.append(jnp.broadcast_to(tt, (rows, t1.shape[1])))
    first = lax.broadcasted_iota(jnp.int32, sfx_tile.shape, 0) % rows < c
    sfx_chunk = sfx_tile - jnp.where(first, jnp.concatenate(t1_rows, axis=0), 0.0)
    pfx = jnp.where(first, jnp.concatenate(t0_rows, axis=0), jnp.concatenate(tt_rows, axis=0))
    kd = (k * jnp.exp(sfx_chunk)).astype(BF16)
    kt = (k * jnp.exp(sfx_tile)).astype(BF16)
    v = _dot_nt(hn, wv_s[...]).astype(BF16)
    q_f = _dot_nt(hn, wq_s[...]) * q_scale
    q = q_f.astype(BF16)
    qt = (q_f * jnp.exp(pfx)).astype(BF16)
    r = _dot_nt(hn, wr_s[...])

    row_c = lax.broadcasted_iota(jnp.int32, (rows, rows), 0) // c
    col_c = lax.broadcasted_iota(jnp.int32, (rows, rows), 1) // c
    outs = []
    for b in range(nb):
        sl = slice(b * rows, (b + 1) * rows)
        s_in = st_ref[b]
        a = jnp.where(row_c == col_c, _dot_nt(q[sl], kd[sl]),
                      jnp.where(row_c > col_c, _dot_nt(q[sl], kt[sl]), 0.0))
        outs.append(_dot_nt(qt[sl], s_in.astype(BF16)) + _dot(a.astype(BF16), v[sl]))
        dec = jnp.exp(pfx[(b + 1) * rows - 1:(b + 1) * rows])
        st_ref[b] = dec * s_in + _dot_tn(v[sl], kt[sl])
    o = jnp.concatenate(outs, axis=0)
    o = o * lax.rsqrt(jnp.mean(o * o, axis=-1, keepdims=True) + EPS) * on_ref[...]
    o = (o * (r * jax.nn.sigmoid(r))).astype(o_ref.dtype)
    o_ref[...] = o.reshape(nb, rows, o.shape[-1])


def _gla_mix(x, g, w_in_t, w_glow_t, w_gate, b_gate, o_norm, layer, *, dk, dv, rows):
    batch, seq, d = x.shape
    kw = GLA_HEADS * dk
    vw = GLA_HEADS * dv
    assert kw % dk == 0 and (2 * kw) % dv == 0 and seq % rows == 0 and rows % GLA_CHUNK == 0
    assert (batch * rows) % MXU_DIM == 0
    k_off = kw // dk
    v_off = 2 * kw // dv
    r_off = (2 * kw + vw) // dv
    gr = w_glow_t.shape[0]
    return pl.pallas_call(
        functools.partial(_gla_kernel, q_scale=dk ** -0.5),
        grid=(GLA_HEADS, seq // rows),
        in_specs=[
            pl.BlockSpec((batch, rows, d), lambda h, s: (0, s, 0)),
            pl.BlockSpec((1, d), lambda h, s: (0, 0)),
            pl.BlockSpec((None, dk, d), lambda h, s: (layer, h, 0)),
            pl.BlockSpec((None, dk, d), lambda h, s: (layer, k_off + h, 0)),
            pl.BlockSpec((None, dv, d), lambda h, s: (layer, v_off + h, 0)),
            pl.BlockSpec((None, dv, d), lambda h, s: (layer, r_off + h, 0)),
            pl.BlockSpec((gr, d), lambda h, s: (0, 0)),
            pl.BlockSpec((gr, dk), lambda h, s: (0, h)),
            pl.BlockSpec((1, dk), lambda h, s: (0, h)),
            pl.BlockSpec((1, dv), lambda h, s: (0, 0)),
        ],
        out_specs=pl.BlockSpec((batch, rows, dv), lambda h, s: (0, s, h)),
        out_shape=jax.ShapeDtypeStruct((batch, seq, vw), BF16),
        scratch_shapes=[
            pltpu.VMEM((batch, dv, dk), F32),
            pltpu.VMEM((dk, d), BF16),
            pltpu.VMEM((dk, d), BF16),
            pltpu.VMEM((dv, d), BF16),
            pltpu.VMEM((dv, d), BF16),
            pltpu.VMEM((gr, d), BF16),
        ],
        compiler_params=_cparams("parallel", "arbitrary"),
        name="gla_mix",
    )(x, g, w_in_t, w_in_t, w_in_t, w_in_t, w_glow_t, w_gate, b_gate, o_norm)


def _s5_pow_table(cs, n_steps):
    ms = list(range(cs + 1))
    d = 1
    while d < n_steps:
        if cs * d not in ms:
            ms.append(cs * d)
        d *= 2
    pad = (-len(ms)) % 8
    return ms + [0] * pad


def _s5_kernel(u_ref, mt_ref, lr_ref, li_ref, ldt_ref, btr_ref, bti_ref, cr_ref, ci_ref, dsk_ref,
               y_ref, w_ref, bc_ref, cct_ref, lhs_ref, q_ref, *, n_steps, pow_ms):
    cs = u_ref.shape[0]
    ns = lr_ref.shape[-1]
    nc = LANES

    lr = jnp.minimum(lr_ref[...], S5_EIG_CLIP)
    li = li_ref[...]
    dt = jnp.exp(ldt_ref[...])
    base_r = lr * dt
    base_i = li * dt
    mt = mt_ref[...]
    mag = jnp.exp(mt * base_r)
    p_re = mag * jnp.cos(mt * base_i)
    p_im = mag * jnp.sin(mt * base_i)

    def apow(m):
        idx = pow_ms.index(m)
        return p_re[idx:idx + 1, :], p_im[idx:idx + 1, :]

    a_re, a_im = apow(1)
    den = lr * lr + li * li
    nr = a_re - 1.0
    f_re = (nr * lr + a_im * li) / den
    f_im = (a_im * lr - nr * li) / den
    btr = btr_ref[...]
    bti = bti_ref[...]
    bd_re = f_re * btr - f_im * bti
    bd_im = f_re * bti + f_im * btr
    c_re = cr_ref[...]
    c_im = ci_ref[...]

    for tau in range(cs):
        pr, pi = apow(tau)
        q_re = bd_re * pr - bd_im * pi
        q_im = bd_re * pi + bd_im * pr
        q_ref[0, tau * nc:(tau + 1) * nc, :] = q_re
        q_ref[1, tau * nc:(tau + 1) * nc, :] = q_im
        s = cs - 1 - tau
        bc_ref[s * nc:(s + 1) * nc, 0:ns] = q_re.astype(BF16)
        bc_ref[s * nc:(s + 1) * nc, ns:2 * ns] = q_im.astype(BF16)

    qr_hi, qr_lo = _split_bf16(q_ref[0])
    qi_hi, qi_lo = _split_bf16(q_ref[1])
    cr_hi, cr_lo = _split_bf16(c_re)
    ci_hi, ci_lo = _split_bf16(c_im)
    kall = (_dot_nt(qr_hi, cr_hi) + _dot_nt(qr_hi, cr_lo) + _dot_nt(qr_lo, cr_hi)
            - _dot_nt(qi_hi, ci_hi) - _dot_nt(qi_hi, ci_lo) - _dot_nt(qi_lo, ci_hi))
    kall = kall.astype(BF16)

    zero_blk = jnp.zeros((nc, nc), BF16)
    for s in range(cs):
        for t in range(cs):
            blk = kall[(t - s) * nc:(t - s + 1) * nc, :] if t >= s else zero_blk
            w_ref[s * nc:(s + 1) * nc, t * nc:(t + 1) * nc] = blk

    for t in range(cs):
        pr, pi = apow(t + 1)
        cct_ref[t * nc:(t + 1) * nc, 0:ns] = (c_re * pr - c_im * pi).astype(BF16)
        cct_ref[t * nc:(t + 1) * nc, ns:2 * ns] = (-(c_re * pi + c_im * pr)).astype(BF16)

    for s in range(cs):
        lhs_ref[:, s * nc:(s + 1) * nc] = u_ref[s].astype(BF16)
    lhs = lhs_ref[...]

    inj = _dot(lhs, bc_ref[...])
    i_re = inj[:, 0:ns]
    i_im = inj[:, ns:2 * ns]
    n_rows = inj.shape[0]
    step = lax.broadcasted_iota(jnp.int32, (n_rows, ns), 0) % n_steps

    def shifted(x, d):
        return jnp.where(step >= d, pltpu.roll(x, d, 0), 0.0)

    d = 1
    while d < n_steps:
        pr, pi = apow(cs * d)
        s_re = shifted(i_re, d)
        s_im = shifted(i_im, d)
        i_re, i_im = (i_re + (pr * s_re - pi * s_im), i_im + (pr * s_im + pi * s_re))
        d *= 2
    x_re = shifted(i_re, 1)
    x_im = shifted(i_im, 1)
    x = jnp.concatenate([x_re, x_im], axis=1).astype(BF16)

    y = _dot(lhs, w_ref[...]) + _dot_nt(x, cct_ref[...])
    dsk = dsk_ref[...]
    for t in range(cs):
        yt = y[:, t * nc:(t + 1) * nc] + dsk * u_ref[t]
        y_ref[t] = jax.nn.gelu(yt).astype(BF16)


def _block_diag_groups(x, gpb):
    g, r, n = x.shape
    nb = g // gpb
    x4 = x.reshape(nb, gpb, r, 1, n)
    eye = jnp.eye(gpb, dtype=bool).reshape(1, gpb, 1, gpb, 1)
    return jnp.where(eye, x4, 0.0).reshape(nb, gpb * r, gpb * n)


def _s5_core(u, lam_re, lam_im, log_dt, b_re, b_im, c_re, c_im, d_skip, *, seq):
    cs, n_rows, width = u.shape
    n_state = lam_re.shape[1]
    gpb = LANES // S5_GROUP
    nblk = width // LANES
    ns = gpb * n_state
    n_steps = seq // cs
    pow_ms = _s5_pow_table(cs, n_steps)

    mtab = jnp.asarray(np.repeat(np.asarray(pow_ms, np.float32)[:, None], ns, axis=1))
    lr = lam_re.reshape(nblk, 1, ns)
    li = lam_im.reshape(nblk, 1, ns)
    ldt = jnp.repeat(log_dt, n_state).reshape(nblk, 1, ns)
    btr = _block_diag_groups(jnp.swapaxes(b_re, 1, 2), gpb)
    bti = _block_diag_groups(jnp.swapaxes(b_im, 1, 2), gpb)
    cr = _block_diag_groups(c_re, gpb)
    ci = _block_diag_groups(c_im, gpb)
    dsk = d_skip.reshape(nblk, 1, LANES)

    row_spec = pl.BlockSpec((None, 1, ns), lambda j: (j, 0, 0))
    mat_spec = pl.BlockSpec((None, LANES, ns), lambda j: (j, 0, 0))
    return pl.pallas_call(
        functools.partial(_s5_kernel, n_steps=n_steps, pow_ms=pow_ms),
        grid=(nblk,),
        in_specs=[
            pl.BlockSpec((cs, n_rows, LANES), lambda j: (0, 0, j)),
            pl.BlockSpec((len(pow_ms), ns), lambda j: (0, 0)),
            row_spec, row_spec, row_spec,
            mat_spec, mat_spec, mat_spec, mat_spec,
            pl.BlockSpec((None, 1, LANES), lambda j: (j, 0, 0)),
        ],
        out_specs=pl.BlockSpec((cs, n_rows, LANES), lambda j: (0, 0, j)),
        out_shape=jax.ShapeDtypeStruct((cs, n_rows, width), BF16),
        scratch_shapes=[
            pltpu.VMEM((cs * LANES, cs * LANES), BF16),
            pltpu.VMEM((cs * LANES, 2 * ns), BF16),
            pltpu.VMEM((cs * LANES, 2 * ns), BF16),
            pltpu.VMEM((n_rows, cs * LANES), BF16),
            pltpu.VMEM((2, cs * LANES, ns), F32),
        ],
        compiler_params=_cparams("parallel"),
        name="s5_core",
    )(u, mtab, lr, li, ldt, btr, bti, cr, ci, dsk)


def kernel(x, gla_norm, gla_w_in, gla_w_gate_up, gla_b_gate, gla_o_norm, gla_w_out, s5_norm, s5_w_in, s5_lam_re, s5_lam_im, s5_log_dt, s5_b_re, s5_b_im, s5_c_re, s5_c_im, s5_d, s5_w_out, mlp_norm, mlp_w_up, mlp_w_down, final_norm):
    batch, seq, d = x.shape
    depth = mlp_norm.shape[0]
    t = batch * seq
    cs = S5_CS
    gate_rank, kw = gla_w_gate_up.shape[1:]
    dv = gla_o_norm.shape[1]
    dk = kw // GLA_HEADS
    main_w = gla_w_in.shape[2] - gate_rank

    gla_w_in_t = jnp.swapaxes(gla_w_in, 1, 2)
    h = x.reshape(t, d)
    fg = final_norm.reshape(1, d)
    for i in range(depth):
        j = i // N_MIXERS
        if i % N_MIXERS == 0:
            w_glow_t = jnp.pad(gla_w_in_t[j, main_w:, :], ((0, LANES - gate_rank), (0, 0)))
            w_gate = jnp.pad(gla_w_gate_up[j], ((0, LANES - gate_rank), (0, 0)))
            og = _gla_mix(h.reshape(batch, seq, d), gla_norm[j].reshape(1, d), gla_w_in_t, w_glow_t, w_gate,
                          gla_b_gate[j].reshape(1, kw), gla_o_norm[j].reshape(1, dv), j,
                          dk=dk, dv=dv, rows=128)
            h = _matmul_res(og.reshape(t, GLA_HEADS * dv), gla_w_out, j, h, tm=1024, tn=1024)
        else:
            h3 = h.reshape(t // cs, cs, d)
            u = _norm_matmul_cm(h3, s5_norm[j].reshape(1, d), s5_w_in, j, tk=1024 // cs)
            y = _s5_core(u, s5_lam_re[j], s5_lam_im[j], s5_log_dt[j],
                         s5_b_re[j], s5_b_im[j], s5_c_re[j], s5_c_im[j], s5_d[j], seq=seq)
            h = _glu_res(y, s5_w_out, j, h3, tk=1024 // cs, tn=1024).reshape(t, d)
        h = _mlp(h, mlp_norm[i].reshape(1, d), mlp_w_up, mlp_w_down, i, fg,
                 final_norm=(i == depth - 1), tm=1024, th=512)
    return h.reshape(batch, seq, d)
```
